```python
import functools
import jax, jax.numpy as jnp
from jax import lax
import numpy as np


D_MODEL = 1024
BATCH = 8
SEQ = 2048
DEPTH = 1
DEC_BATCH = 32
DEC_SEQ = 1
PAST_LEN = 8192
PAGE_SIZE = 128

ATT_W = D_MODEL // 2
HEAD_DIM = 64
ATT_H = ATT_W // HEAD_DIM
IDX_H = 16
IDX_D = 64
IDX_W_SCALE = (IDX_H ** -0.5) * (IDX_D ** -0.5)
TOPK_MAX = 256
Q_BLOCK = 128
RW_W = D_MODEL - ATT_W
RW_N = 64
RW_H = RW_W // RW_N
W_LORA = 64
A_LORA = 64
G_LORA = 128
GN_EPS = 64e-5
MIX_W = ATT_W + RW_W
ATT_COLS = 3 * ATT_W + IDX_H * IDX_D + IDX_D + IDX_H
RW_COLS = 3 * RW_W + W_LORA + A_LORA + G_LORA
IN_COLS = ATT_COLS + RW_COLS
FFN_DIM = ((8 * D_MODEL + 3 * 256 - 1) // (3 * 256)) * 256
RMS_EPS = 1e-6

kernel_name = 'hymba_dsa_rwkv7_adaln_step'


def rms_norm(x, g=None, eps=RMS_EPS):
    xf = x.astype(jnp.float32)
    y = xf * lax.rsqrt(jnp.mean(xf * xf, axis=-1, keepdims=True) + eps)
    if g is not None:
        y = y * g.astype(jnp.float32)
    return y.astype(x.dtype)


def gather_rows(src, idx):
    return jax.vmap(lambda s, i: s[i])(src, idx)


def indexer_scores(qi, ki, wi):
    dots = jnp.einsum('bqhd,bld->bqhl', qi.astype(jnp.float32), ki.astype(jnp.float32))
    return jnp.einsum('bqhl,bqh->bql', jax.nn.relu(dots), wi.astype(jnp.float32))


def attend_selected(q, kg, vg, valid):
    s = jnp.einsum('bqhd,bqkhd->bqhk', q.astype(jnp.float32), kg.astype(jnp.float32)) * (HEAD_DIM ** -0.5)
    s = jnp.where(valid[:, :, None, :], s, -jnp.inf)
    p = jax.nn.softmax(s, axis=-1)
    o = jnp.einsum('bqhk,bqkhd->bqhd', p, vg.astype(jnp.float32))
    return o.astype(q.dtype)


def prompt_attention(q, k, v, qi, ki, wi):
    B, T = q.shape[:2]
    n_sel = min(TOPK_MAX, T // 4)
    nb = T // Q_BLOCK

    def blocks(z):
        return jnp.moveaxis(z.reshape((B, nb, Q_BLOCK) + z.shape[2:]), 1, 0)

    def one_block(args):
        qb, qib, wib, t0 = args
        tpos = t0 + jnp.arange(Q_BLOCK)
        score = indexer_scores(qib, ki, wib)
        allowed = jnp.arange(T)[None, :] <= tpos[:, None]
        score = jnp.where(allowed[None], score, -jnp.inf)
        _, idx = lax.top_k(score, n_sel)
        valid = idx <= tpos[None, :, None]
        return attend_selected(qb, gather_rows(k, idx), gather_rows(v, idx), valid)

    out = lax.map(one_block, (blocks(q), blocks(qi), blocks(wi), jnp.arange(nb, dtype=jnp.int32) * Q_BLOCK))
    return jnp.moveaxis(out, 0, 1).reshape(B, T, ATT_H, HEAD_DIM)


def sample_attention(q, k_new, v_new, qi, ki_new, wi, cache_k, cache_v, cache_ik, page_table):
    B, T = q.shape[:2]
    past = page_table.shape[1] * PAGE_SIZE
    L = past + T
    n_sel = min(TOPK_MAX, L // 4)
    past_ik = cache_ik[page_table].reshape(B, past, IDX_D)
    ki_all = jnp.concatenate([past_ik.astype(ki_new.dtype), ki_new], axis=1)
    tpos = past + jnp.arange(T)
    score = indexer_scores(qi, ki_all, wi)
    allowed = jnp.arange(L)[None, :] <= tpos[:, None]
    score = jnp.where(allowed[None], score, -jnp.inf)
    _, idx = lax.top_k(score, n_sel)
    valid = idx <= tpos[None, :, None]
    is_past = idx < past
    pidx = jnp.minimum(idx, past - 1)
    page = jnp.take_along_axis(page_table, (pidx // PAGE_SIZE).reshape(B, -1), axis=1).reshape(idx.shape)
    row = page * PAGE_SIZE + pidx % PAGE_SIZE
    nidx = jnp.clip(idx - past, 0, T - 1)

    def pick(pool, new):
        flat = pool.reshape((-1,) + pool.shape[2:])
        return jnp.where(is_past[..., None, None], flat[row].astype(new.dtype), gather_rows(new, nidx))

    return attend_selected(q, pick(cache_k, k_new), pick(cache_v, v_new), valid)


def wkv7_step(S, inp):
    r, w, k, v, kk, a = inp
    sa = jnp.einsum('bhij,bhj->bhi', S, -kk)
    S = S * w[:, :, None, :] + sa[..., None] * (kk * a)[:, :, None, :] + v[..., None] * k[:, :, None, :]
    return S, jnp.einsum('bhij,bhj->bhi', S, r)


def rwkv7_mix(pr, prev_row, s0, lp):
    B, T, _ = pr.shape
    prev = jnp.concatenate([prev_row[:, None, :].astype(pr.dtype), pr[:, :-1]], axis=1)
    xm = pr + (prev - pr) * lp['rw_mu']
    p0 = 3 * RW_W
    r = xm[..., :RW_W]
    k = xm[..., RW_W:2 * RW_W]
    v = xm[..., 2 * RW_W:p0]
    wd = xm[..., p0:p0 + W_LORA]
    ad = xm[..., p0 + W_LORA:p0 + W_LORA + A_LORA]
    gd = xm[..., p0 + W_LORA + A_LORA:]
    w_log = -jax.nn.softplus(-(lp['rw_w0'] + jnp.tanh(wd) @ lp['rw_w2'])) - 0.5
    a = jax.nn.sigmoid(lp['rw_a0'] + ad @ lp['rw_a2'])
    g = jax.nn.sigmoid(gd) @ lp['rw_g2']

    def heads(z):
        return z.astype(jnp.float32).reshape(B, T, RW_H, RW_N)

    kk = heads(k * lp['rw_k_k'])
    kk = kk * lax.rsqrt(jnp.maximum(jnp.sum(kk * kk, axis=-1, keepdims=True), 1e-24))
    k = k * (1 + (a - 1) * lp['rw_k_a'])
    rh, kh, vh, ah = heads(r), heads(k), heads(v), heads(a)
    decay = jnp.exp(-jnp.exp(heads(w_log)))
    xs = tuple(jnp.moveaxis(z, 1, 0) for z in (rh, decay, kh, vh, kk, ah))
    s_fin, ys = lax.scan(wkv7_step, s0.astype(jnp.float32), xs)
    y = jnp.moveaxis(ys, 0, 1)
    mu = jnp.mean(y, axis=-1, keepdims=True)
    var = jnp.mean((y - mu) ** 2, axis=-1, keepdims=True)
    yn = ((y - mu) * lax.rsqrt(var + GN_EPS)).reshape(B, T, RW_W) * lp['rw_ln_w'].astype(jnp.float32) + lp['rw_ln_b'].astype(jnp.float32)
    bonus = (jnp.sum(rh * kh * lp['rw_r_k'].astype(jnp.float32), axis=-1, keepdims=True) * vh).reshape(B, T, RW_W)
    out = ((yn + bonus) * g.astype(jnp.float32)).astype(pr.dtype)
    return out, s_fin, pr[:, -1]


def decoder_layer(x, c, attn_fn, prev_row, s0, lp):
    B, T, _ = x.shape
    mod = (jax.nn.silu(c) @ lp['w_ada'] + lp['b_ada'])[:, None, :]
    shift1, scale1, gate1, shift2, scale2, gate2 = jnp.split(mod, 6, axis=-1)
    h = rms_norm(x, lp['norm1_g']) * (1 + scale1) + shift1
    proj = h @ lp['w_in']
    pa, pr = proj[..., :ATT_COLS], proj[..., ATT_COLS:]
    o1, o2, o3 = ATT_W, 2 * ATT_W, 3 * ATT_W
    o4 = o3 + IDX_H * IDX_D
    o5 = o4 + IDX_D
    q = rms_norm(pa[..., :o1].reshape(B, T, ATT_H, HEAD_DIM), lp['q_norm_g'])
    k = rms_norm(pa[..., o1:o2].reshape(B, T, ATT_H, HEAD_DIM), lp['k_norm_g'])
    v = pa[..., o2:o3].reshape(B, T, ATT_H, HEAD_DIM)
    qi = pa[..., o3:o4].reshape(B, T, IDX_H, IDX_D)
    ki = rms_norm(pa[..., o4:o5])
    wi = pa[..., o5:ATT_COLS] * IDX_W_SCALE
    att = attn_fn(q, k, v, qi, ki, wi).reshape(B, T, ATT_W)
    rw, s_fin, last_row = rwkv7_mix(pr, prev_row, s0, lp)
    x = x + gate1 * (jnp.concatenate([att, rw], axis=-1) @ lp['w_out'])
    h2 = rms_norm(x, lp['norm2_g']) * (1 + scale2) + shift2
    ffn = (jax.nn.silu(h2 @ lp['w_ffn_gate']) * (h2 @ lp['w_ffn_up'])) @ lp['w_ffn_down']
    x = x + gate2 * ffn
    return x, (k, v, ki, s_fin, last_row)


def setup_inputs(seed: int = 0) -> dict:
    key = jax.random.key(seed)
    ks = iter(jax.random.split(key, 48))
    nrm = lambda shape, s=1.0: jax.random.normal(next(ks), shape, jnp.float32) * s
    n_pages = PAST_LEN // PAGE_SIZE
    n_used = DEC_BATCH * n_pages
    n_pool = n_used + max(1, n_used // 4)
    page_table = jax.random.permutation(next(ks), n_pool)[:n_used].reshape(DEC_BATCH, n_pages).astype(jnp.int32)
    return {
        'x_prompt': nrm((BATCH, SEQ, D_MODEL)),
        'x_sample': nrm((DEC_BATCH, DEC_SEQ, D_MODEL)),
        'cache_k': nrm((DEPTH, n_pool, PAGE_SIZE, ATT_H, HEAD_DIM)),
        'cache_v': nrm((DEPTH, n_pool, PAGE_SIZE, ATT_H, HEAD_DIM)),
        'cache_idx_k': nrm((DEPTH, n_pool, PAGE_SIZE, IDX_D)),
        'state_wkv': nrm((DEPTH, DEC_BATCH, RW_H, RW_N, RW_N), 0.3),
        'state_shift': nrm((DEPTH, DEC_BATCH, RW_COLS)),
        'page_table': page_table,
        'c_prompt': nrm((BATCH, D_MODEL)),
        'c_sample': nrm((DEC_BATCH, D_MODEL)),
        'norm1_g': 1.0 + nrm((DEPTH, D_MODEL), 0.05),
        'norm2_g': 1.0 + nrm((DEPTH, D_MODEL), 0.05),
        'w_ada': nrm((DEPTH, D_MODEL, 6 * D_MODEL), 0.3 * D_MODEL ** -0.5),
        'b_ada': nrm((DEPTH, 6 * D_MODEL), 0.02),
        'w_in': nrm((DEPTH, D_MODEL, IN_COLS), D_MODEL ** -0.5),
        'q_norm_g': 1.0 + nrm((DEPTH, HEAD_DIM), 0.05),
        'k_norm_g': 1.0 + nrm((DEPTH, HEAD_DIM), 0.05),
        'rw_mu': jax.random.uniform(next(ks), (DEPTH, RW_COLS), jnp.float32),
        'rw_w0': nrm((DEPTH, RW_W), 0.5),
        'rw_w2': nrm((DEPTH, W_LORA, RW_W), 0.5 * W_LORA ** -0.5),
        'rw_a0': nrm((DEPTH, RW_W), 0.1),
        'rw_a2': nrm((DEPTH, A_LORA, RW_W), 0.5 * A_LORA ** -0.5),
        'rw_g2': nrm((DEPTH, G_LORA, RW_W), G_LORA ** -0.5),
        'rw_k_k': 0.85 + nrm((DEPTH, RW_W), 0.05),
        'rw_k_a': 1.0 + nrm((DEPTH, RW_W), 0.05),
        'rw_r_k': nrm((DEPTH, RW_H, RW_N), 0.1),
        'rw_ln_w': 1.0 + nrm((DEPTH, RW_W), 0.05),
        'rw_ln_b': nrm((DEPTH, RW_W), 0.02),
        'w_out': nrm((DEPTH, MIX_W, D_MODEL), MIX_W ** -0.5),
        'w_ffn_gate': nrm((DEPTH, D_MODEL, FFN_DIM), D_MODEL ** -0.5),
        'w_ffn_up': nrm((DEPTH, D_MODEL, FFN_DIM), D_MODEL ** -0.5),
        'w_ffn_down': nrm((DEPTH, FFN_DIM, D_MODEL), FFN_DIM ** -0.5),
    }


def reference(x_prompt, x_sample, cache_k, cache_v, cache_idx_k, state_wkv, state_shift, page_table,
              c_prompt, c_sample, norm1_g, norm2_g, w_ada, b_ada, w_in, q_norm_g, k_norm_g,
              rw_mu, rw_w0, rw_w2, rw_a0, rw_a2, rw_g2, rw_k_k, rw_k_a, rw_r_k, rw_ln_w, rw_ln_b,
              w_out, w_ffn_gate, w_ffn_up, w_ffn_down):
    y_p, y_s = x_prompt, x_sample
    Bp = x_prompt.shape[0]
    kp, vp, ikp, wkvp, shp = [], [], [], [], []
    ksm, vsm, iks, wkvs, shs = [], [], [], [], []
    for l in range(DEPTH):
        lp = {
            'norm1_g': norm1_g[l], 'norm2_g': norm2_g[l], 'w_ada': w_ada[l], 'b_ada': b_ada[l],
            'w_in': w_in[l], 'q_norm_g': q_norm_g[l], 'k_norm_g': k_norm_g[l],
            'rw_mu': rw_mu[l], 'rw_w0': rw_w0[l], 'rw_w2': rw_w2[l], 'rw_a0': rw_a0[l], 'rw_a2': rw_a2[l],
            'rw_g2': rw_g2[l], 'rw_k_k': rw_k_k[l], 'rw_k_a': rw_k_a[l], 'rw_r_k': rw_r_k[l],
            'rw_ln_w': rw_ln_w[l], 'rw_ln_b': rw_ln_b[l], 'w_out': w_out[l],
            'w_ffn_gate': w_ffn_gate[l], 'w_ffn_up': w_ffn_up[l], 'w_ffn_down': w_ffn_down[l],
        }
        prev0 = jnp.zeros((Bp, RW_COLS), x_prompt.dtype)
        s00 = jnp.zeros((Bp, RW_H, RW_N, RW_N), jnp.float32)
        y_p, st_p = decoder_layer(y_p, c_prompt, prompt_attention, prev0, s00, lp)
        samp_attn = functools.partial(sample_attention, cache_k=cache_k[l], cache_v=cache_v[l],
                                      cache_ik=cache_idx_k[l], page_table=page_table)
        y_s, st_s = decoder_layer(y_s, c_sample, samp_attn, state_shift[l], state_wkv[l], lp)
        kp.append(st_p[0]); vp.append(st_p[1]); ikp.append(st_p[2]); wkvp.append(st_p[3]); shp.append(st_p[4])
        ksm.append(st_s[0]); vsm.append(st_s[1]); iks.append(st_s[2]); wkvs.append(st_s[3]); shs.append(st_s[4])
    return (y_p, y_s,
            jnp.stack(kp), jnp.stack(vp), jnp.stack(ikp), jnp.stack(wkvp), jnp.stack(shp),
            jnp.stack(ksm), jnp.stack(vsm), jnp.stack(iks), jnp.stack(wkvs), jnp.stack(shs))
```

```python
import functools

import jax
import jax.numpy as jnp
from jax import lax
from jax.experimental import pallas as pl
from jax.experimental.pallas import tpu as pltpu

F32 = jnp.float32
BF = jnp.bfloat16
I32 = jnp.int32

LANES = 128
HEAD_DIM = 64
ATT_H = 8
ATT_W = ATT_H * HEAD_DIM
IDX_H = 16
IDX_D = 64
IDX_W_SCALE = (IDX_H ** -0.5) * (IDX_D ** -0.5)
RW_N = 64
RW_H = 8
RW_W = RW_H * RW_N
W_LORA = 64
A_LORA = 64
G_LORA = 128
TOPK_MAX = 256
PAGE_SIZE = 128
RMS_EPS = 1e-6
GN_EPS = 64e-5
NEG_BIG = -1e30
INT_MIN = -2147483648

C_Q, C_K, C_V, C_QI = 0, ATT_W, 2 * ATT_W, 3 * ATT_W
C_KIW = C_QI + IDX_H * IDX_D
C_PR = C_KIW + LANES
ATT_COLS = 3 * ATT_W + IDX_H * IDX_D + IDX_D + IDX_H
RW_COLS = 3 * RW_W + W_LORA + A_LORA + G_LORA
IN_COLS_PAD = C_PR + RW_COLS

NT_DIMS = (((1,), (1,)), ((), ()))


def _params(sem, vmem_mb=48):
    return pltpu.CompilerParams(dimension_semantics=sem, vmem_limit_bytes=vmem_mb * 1024 * 1024)


def _split_dot(z, ones_bd):
    zh = z.astype(BF)
    zl = (z - zh.astype(F32)).astype(BF)
    return (jnp.dot(zh, ones_bd, preferred_element_type=F32)
            + jnp.dot(zl, ones_bd, preferred_element_type=F32))


def _sigmoid(x):
    return 1.0 / (1.0 + jnp.exp(-x))


def _mod_kernel(c_ref, w_ref, b_ref, o_ref):
    c = c_ref[...]
    s = c * _sigmoid(c)
    o_ref[...] = jnp.dot(s.astype(BF), w_ref[...].astype(BF), preferred_element_type=F32) + b_ref[...]


def _adaln(c_all, w_ada, b_ada):
    n, d = c_all.shape
    d6 = w_ada.shape[1]
    return pl.pallas_call(
        _mod_kernel,
        grid=(d6 // d,),
        in_specs=[pl.BlockSpec((n, d), lambda j: (0, 0)),
                  pl.BlockSpec((d, d), lambda j: (0, j)),
                  pl.BlockSpec((1, d), lambda j: (0, j))],
        out_specs=pl.BlockSpec((n, d), lambda j: (0, j)),
        out_shape=jax.ShapeDtypeStruct((n, d6), F32),
        compiler_params=_params(("arbitrary",)),
        name="adaln_mod",
    )(c_all, w_ada, b_ada.reshape(1, d6))


def _mod_spec(per_batch, chunk, tiles_per_batch, tm, d):
    if per_batch:
        return pl.BlockSpec((None, 1, d), lambda i: (i // tiles_per_batch, 0, chunk))
    return pl.BlockSpec((tm, d), lambda i: (i, chunk))


def _inproj_kernel(x_ref, sh_ref, sc_ref, g1_ref, w_ref, qg_ref, kg_ref, bd_ref,
                   q_o, k_o, v_o, ke_o, ko_o, ve_o, vo_o, qi_o, kiw_o, kie_o, kio_o, pr_o):
    x = x_ref[...]
    ms = jnp.mean(x * x, axis=-1, keepdims=True)
    h = x * lax.rsqrt(ms + RMS_EPS) * g1_ref[...]
    h = h * (1.0 + sc_ref[...]) + sh_ref[...]
    hb = h.astype(BF)

    def mm(lo, hi):
        return jnp.dot(hb, w_ref[:, lo:hi], preferred_element_type=F32)

    bd = bd_ref[...]

    def headnorm(z, g):
        ss = _split_dot(z * z, bd)
        return z * lax.rsqrt(ss * (1.0 / HEAD_DIM) + RMS_EPS) * g

    lane = lax.broadcasted_iota(I32, (1, ATT_W), 1)
    even = (lane & HEAD_DIM) == 0

    q = headnorm(mm(C_Q, C_K), qg_ref[...]) * (HEAD_DIM ** -0.5)
    q_o[...] = q.astype(BF)
    k = headnorm(mm(C_K, C_V), kg_ref[...])
    k_o[...] = k
    ke_o[...] = jnp.where(even, k, 0.0).astype(BF)
    ko_o[...] = jnp.where(even, 0.0, k).astype(BF)
    v = mm(C_V, C_QI)
    v_o[...] = v
    ve_o[...] = jnp.where(even, v, 0.0).astype(BF)
    vo_o[...] = jnp.where(even, 0.0, v).astype(BF)
    qi_o[...] = mm(C_QI, C_KIW).astype(BF)

    kw = mm(C_KIW, C_PR)
    lane1 = lax.broadcasted_iota(I32, (1, LANES), 1)
    is_key = lane1 < IDX_D
    ss = jnp.sum(jnp.where(is_key, kw * kw, 0.0), axis=-1, keepdims=True)
    kin = kw * lax.rsqrt(ss * (1.0 / IDX_D) + RMS_EPS)
    kiw_o[...] = jnp.where(is_key, kin, kw * IDX_W_SCALE)
    ki_only = jnp.where(is_key, kin, 0.0)
    kie_o[...] = ki_only.astype(BF)
    kio_o[...] = pltpu.roll(ki_only, IDX_D, axis=1).astype(BF)
    pr_o[...] = mm(C_PR, IN_COLS_PAD)


def _inproj(x2d, mod, per_batch, tiles_per_batch, tm, norm1_g, w_in_p, qg, kg, bd512):
    n, d = x2d.shape
    row = lambda w: pl.BlockSpec((tm, w), lambda i: (i, 0))
    const = lambda a: pl.BlockSpec(a.shape, lambda i: (0,) * a.ndim)
    outs = [(ATT_W, BF), (ATT_W, F32), (ATT_W, F32), (ATT_W, BF), (ATT_W, BF), (ATT_W, BF), (ATT_W, BF),
            (IDX_H * IDX_D, BF), (LANES, F32), (LANES, BF), (LANES, BF), (RW_COLS, F32)]
    return pl.pallas_call(
        _inproj_kernel,
        grid=(n // tm,),
        in_specs=[row(d),
                  _mod_spec(per_batch, 0, tiles_per_batch, tm, d),
                  _mod_spec(per_batch, 1, tiles_per_batch, tm, d),
                  const(norm1_g), const(w_in_p), const(qg), const(kg), const(bd512)],
        out_specs=[row(w) for w, _ in outs],
        out_shape=[jax.ShapeDtypeStruct((n, w), dt) for w, dt in outs],
        compiler_params=_params(("arbitrary",), 56),
        name="inproj",
    )(x2d, mod, mod, norm1_g, w_in_p, qg, kg, bd512)


def _sortable_key(s):
    s = jnp.where(s == 0.0, 0.0, s)
    bits = lax.bitcast_convert_type(s, I32)
    return jnp.where(bits < 0, bits ^ 0x7FFFFFFF, bits)


def _bisect_bits(count_ge, n_sel, shape):
    def body(it, thr_u):
        cand_u = thr_u | lax.shift_left(jnp.int32(1), 31 - it)
        cnt = count_ge(cand_u ^ INT_MIN)
        return jnp.where(cnt >= n_sel, cand_u, thr_u)
    thr_u = lax.fori_loop(0, 32, body, jnp.zeros(shape, I32))
    return thr_u ^ INT_MIN


QB = 256


def _attn_prompt_kernel(q_ref, qi_ref, kiwq_ref, ke_ref, ko_ref, ve_ref, vo_ref, kie_ref, kio_ref, o_ref,
                        key_scr, sc_scr, m_scr, l_scr, acc_scr, *, n_sel):
    i = pl.program_id(1)
    nch = i + 1
    wt = kiwq_ref[...].T
    t_glob = i * QB + lax.broadcasted_iota(I32, (QB, QB), 1)
    s_loc = lax.broadcasted_iota(I32, (QB, QB), 0)

    def score_chunk(j, c):
        off = pl.multiple_of(j * QB, QB)
        ke = kie_ref[pl.ds(off, QB), :]
        ko = kio_ref[pl.ds(off, QB), :]
        for hp in range(IDX_H // 2):
            qt = qi_ref[:, hp * LANES:(hp + 1) * LANES]
            d0 = lax.dot_general(ke, qt, NT_DIMS, preferred_element_type=F32)
            d1 = lax.dot_general(ko, qt, NT_DIMS, preferred_element_type=F32)
            r0 = IDX_D + 2 * hp
            contrib = (wt[r0:r0 + 1, :] * jnp.maximum(d0, 0.0)
                       + wt[r0 + 1:r0 + 2, :] * jnp.maximum(d1, 0.0))
            if hp == 0:
                sc_scr[...] = contrib
            else:
                sc_scr[...] += contrib
        allowed = (off + s_loc) <= t_glob
        key_scr[pl.ds(off, QB), :] = _sortable_key(jnp.where(allowed, sc_scr[...], -jnp.inf))
        return c
    lax.fori_loop(0, nch, score_chunk, 0)

    def count_ge(cand):
        def body(j, acc):
            off = pl.multiple_of(j * QB, QB)
            ind = jnp.where(key_scr[pl.ds(off, QB), :] >= cand, 1.0, 0.0)
            return acc + ind.reshape(QB // 8, 8, QB).sum(axis=0)
        acc = lax.fori_loop(0, nch, body, jnp.zeros((8, QB), F32))
        return acc.sum(axis=0, keepdims=True)

    thr = _bisect_bits(count_ge, float(n_sel), (1, QB))

    m_scr[...] = jnp.full(m_scr.shape, NEG_BIG, F32)
    l_scr[...] = jnp.zeros(l_scr.shape, F32)
    acc_scr[...] = jnp.zeros(acc_scr.shape, F32)

    def attn_chunk(j, c):
        off = pl.multiple_of(j * QB, QB)
        allowed = (off + s_loc) <= t_glob
        sel_t = (key_scr[pl.ds(off, QB), :] >= thr) & allowed
        bias = jnp.where(sel_t, 0.0, NEG_BIG).T
        for h in range(ATT_H):
            p, par = divmod(h, 2)
            cols = slice(p * LANES, (p + 1) * LANES)
            k_src, v_src = (ke_ref, ve_ref) if par == 0 else (ko_ref, vo_ref)
            s = lax.dot_general(q_ref[:, cols], k_src[pl.ds(off, QB), cols], NT_DIMS,
                                preferred_element_type=F32) + bias
            m_prev = m_scr[h]
            m_new = jnp.maximum(m_prev, jnp.max(s, axis=1, keepdims=True))
            alpha = jnp.exp(m_prev - m_new)
            pexp = jnp.exp(s - m_new)
            l_scr[h] = alpha * l_scr[h] + jnp.sum(pexp, axis=1, keepdims=True)
            acc_scr[h] = alpha * acc_scr[h] + jnp.dot(pexp.astype(BF), v_src[pl.ds(off, QB), cols],
                                                      preferred_element_type=F32)
            m_scr[h] = m_new
        return c
    lax.fori_loop(0, nch, attn_chunk, 0)

    for p in range(ATT_H // 2):
        o = acc_scr[2 * p] / l_scr[2 * p] + acc_scr[2 * p + 1] / l_scr[2 * p + 1]
        o_ref[:, p * LANES:(p + 1) * LANES] = o.astype(o_ref.dtype)


def _attn_prompt(q_bf, qi_bf, kiw, ke, ko, ve, vo, kie, kio, batch, seq):
    n = batch * seq
    nq = seq // QB
    n_sel = min(TOPK_MAX, seq // 4)
    qrow = lambda w: pl.BlockSpec((QB, w), lambda b, i: (b * nq + i, 0))
    kv = lambda w: pl.BlockSpec((seq, w), lambda b, i: (b, 0))
    return pl.pallas_call(
        functools.partial(_attn_prompt_kernel, n_sel=n_sel),
        grid=(batch, nq),
        in_specs=[qrow(ATT_W), qrow(IDX_H * IDX_D), qrow(LANES),
                  kv(ATT_W), kv(ATT_W), kv(ATT_W), kv(ATT_W), kv(LANES), kv(LANES)],
        out_specs=qrow(ATT_W),
        out_shape=jax.ShapeDtypeStruct((n, ATT_W), BF),
        scratch_shapes=[pltpu.VMEM((seq, QB), I32),
                        pltpu.VMEM((QB, QB), F32),
                        pltpu.VMEM((ATT_H, QB, 1), F32),
                        pltpu.VMEM((ATT_H, QB, 1), F32),
                        pltpu.VMEM((ATT_H, QB, LANES), F32)],
        compiler_params=_params(("arbitrary", "arbitrary"), 48),
        name="attn_prompt",
    )(q_bf, qi_bf, kiw, ke, ko, ve, vo, kie, kio)


PG = 8


def _samp_score_kernel(pt_ref, qi_ref, w_ref, *rest):
    page_refs, o_ref = rest[:PG], rest[PG]
    qi = qi_ref[...]
    w = w_ref[...]
    for g in range(PG):
        kp = page_refs[g][...].astype(BF)
        d = lax.dot_general(qi, kp, NT_DIMS, preferred_element_type=F32)
        o_ref[:, g * PAGE_SIZE:(g + 1) * PAGE_SIZE] = jnp.sum(w * jnp.maximum(d, 0.0), axis=0, keepdims=True)


def _samp_scores(page_flat, qi3, w3, cache_ik, n_pages):
    bs = qi3.shape[0]
    ng = n_pages // PG

    def page_spec(g):
        return pl.BlockSpec((None, PAGE_SIZE, IDX_D),
                            lambda b, j, pt: (pt[b * n_pages + j * PG + g], 0, 0))
    grid_spec = pltpu.PrefetchScalarGridSpec(
        num_scalar_prefetch=1,
        grid=(bs, ng),
        in_specs=[pl.BlockSpec((None, IDX_H, IDX_D), lambda b, j, pt: (b, 0, 0)),
                  pl.BlockSpec((None, IDX_H, 1), lambda b, j, pt: (b, 0, 0))]
                 + [page_spec(g) for g in range(PG)],
        out_specs=pl.BlockSpec((None, None, 1, PG * PAGE_SIZE), lambda b, j, pt: (b, j, 0, 0)),
    )
    return pl.pallas_call(
        _samp_score_kernel,
        grid_spec=grid_spec,
        out_shape=jax.ShapeDtypeStruct((bs, ng, 1, PG * PAGE_SIZE), F32),
        compiler_params=_params(("arbitrary", "arbitrary"), 32),
        name="samp_scores",
    )(page_flat, qi3, w3, *([cache_ik] * PG))


def _samp_select_kernel(sc_ref, qi_ref, kiw_ref, rep_ref, hsum_ref, o_ref, *, n_sel, past):
    kiw = kiw_ref[...]
    lane = lax.broadcasted_iota(I32, kiw.shape, 1)
    ki = jnp.where(lane < IDX_D, kiw, 0.0).astype(BF)
    ki_rep = jnp.dot(ki, rep_ref[...], preferred_element_type=F32)
    prod = qi_ref[...].astype(F32) * ki_rep
    d_new = _split_dot(prod, hsum_ref[...])
    w_new = pltpu.roll(kiw, LANES - IDX_D, axis=1)
    s_new = jnp.sum(jnp.where(lane < IDX_H, w_new * jnp.maximum(d_new, 0.0), 0.0), axis=1, keepdims=True)
    new_chunk = jnp.where(lane == 0, s_new, -jnp.inf)
    key = _sortable_key(jnp.concatenate([sc_ref[...], new_chunk], axis=1))
    pos = lax.broadcasted_iota(I32, key.shape, 1)

    def count_ge(cand):
        return jnp.sum(jnp.where(key >= cand, 1.0, 0.0), axis=1, keepdims=True)
    thr = _bisect_bits(count_ge, float(n_sel), (key.shape[0], 1))

    gt = key > thr
    eq = key == thr
    need = float(n_sel) - jnp.sum(jnp.where(gt, 1.0, 0.0), axis=1, keepdims=True)
    nbits = max(1, int(key.shape[1]).bit_length())

    def cut_body(it, cut):
        cand = cut | lax.shift_left(jnp.int32(1), nbits - 1 - it)
        cnt = jnp.sum(jnp.where(eq & (pos < cand), 1.0, 0.0), axis=1, keepdims=True)
        return jnp.where(cnt <= need, cand, cut)
    cut = lax.fori_loop(0, nbits, cut_body, jnp.zeros((key.shape[0], 1), I32))
    sel = (gt | (eq & (pos < cut))) & (pos <= past)
    o_ref[...] = jnp.where(sel, 0.0, NEG_BIG)


def _samp_select(scores, qi_s, kiw_s, rep, hsum, past):
    bs = scores.shape[0]
    n_sel = min(TOPK_MAX, (past + 1) // 4)
    full = lambda a: pl.BlockSpec(a.shape, lambda i: (0,) * a.ndim)
    return pl.pallas_call(
        functools.partial(_samp_select_kernel, n_sel=n_sel, past=past),
        grid=(1,),
        in_specs=[full(scores), full(qi_s), full(kiw_s), full(rep), full(hsum)],
        out_specs=pl.BlockSpec((bs, past + LANES), lambda i: (0, 0)),
        out_shape=jax.ShapeDtypeStruct((bs, past + LANES), F32),
        compiler_params=_params(("arbitrary",), 32),
        name="samp_select",
    )(scores, qi_s, kiw_s, rep, hsum)


def _row_to_col(row, eye):
    return jnp.sum(jnp.where(eye, row, 0.0), axis=1, keepdims=True)


def _expand_heads(x):
    lane = lax.broadcasted_iota(I32, (1, LANES), 1)
    tiles = []
    for p in range(ATT_H // 2):
        a = jnp.sum(jnp.where(lane == 2 * p, x, 0.0), axis=1, keepdims=True)
        b = jnp.sum(jnp.where(lane == 2 * p + 1, x, 0.0), axis=1, keepdims=True)
        tiles.append(jnp.where(lane < HEAD_DIM, a, b))
    return jnp.concatenate(tiles, axis=1)


def _samp_attn_kernel(pt_ref, q_ref, bias_ref, kn_ref, vn_ref, ex_ref, *rest,
                      n_pages):
    k_refs, v_refs = rest[:PG], rest[PG:2 * PG]
    o_ref = rest[2 * PG]
    qbd_scr, m_scr, l_scr, acc_scr = rest[2 * PG + 1:]
    j = pl.program_id(1)
    eye = (lax.broadcasted_iota(I32, (LANES, LANES), 0) == lax.broadcasted_iota(I32, (LANES, LANES), 1))

    @pl.when(j == 0)
    def _():
        r = lax.broadcasted_iota(I32, (LANES, LANES), 0)
        c = lax.broadcasted_iota(I32, (LANES, LANES), 1)
        for t in range(ATT_W // LANES):
            col = _row_to_col(q_ref[:, t * LANES:(t + 1) * LANES].astype(F32), eye)
            qbd_scr[t * LANES:(t + 1) * LANES, :] = jnp.where((t * LANES + r) // HEAD_DIM == c, col, 0.0).astype(BF)
        m_scr[...] = jnp.full(m_scr.shape, NEG_BIG, F32)
        l_scr[...] = jnp.zeros(l_scr.shape, F32)
        acc_scr[...] = jnp.zeros(acc_scr.shape, F32)

    qbd = qbd_scr[...]
    ex = ex_ref[...]
    s_list = []
    for g in range(PG):
        kp = k_refs[g][...].astype(BF)
        s = jnp.dot(kp, qbd, preferred_element_type=F32)
        brow = bias_ref[pl.ds(j * PG + g, 1), :]
        s_list.append(s + _row_to_col(brow, eye))
    m_prev = m_scr[...]
    m_new = m_prev
    for s in s_list:
        m_new = jnp.maximum(m_new, jnp.max(s, axis=0, keepdims=True))
    alpha = jnp.exp(m_prev - m_new)
    l_new = alpha * l_scr[...]
    part = jnp.zeros((8, ATT_W), F32)
    for g in range(PG):
        pexp = jnp.exp(s_list[g] - m_new)
        l_new = l_new + jnp.sum(pexp, axis=0, keepdims=True)
        pe = jnp.dot(pexp.astype(BF), ex, preferred_element_type=F32)
        part = part + (pe * v_refs[g][...]).reshape(PAGE_SIZE // 8, 8, ATT_W).sum(axis=0)
    acc_scr[...] = _expand_heads(alpha) * acc_scr[...] + part.sum(axis=0, keepdims=True)
    l_scr[...] = l_new
    m_scr[...] = m_new

    @pl.when(j == pl.num_programs(1) - 1)
    def _():
        kn = jnp.broadcast_to(kn_ref[...], (8, ATT_W)).astype(BF)
        s_n = jnp.dot(kn, qbd, preferred_element_type=F32)[0:1, :]
        brow = bias_ref[pl.ds(n_pages, 1), :]
        lane = lax.broadcasted_iota(I32, (1, LANES), 1)
        b_n = jnp.sum(jnp.where(lane == 0, brow, 0.0), axis=1, keepdims=True)
        s_n = s_n + b_n
        m_prev = m_scr[...]
        m_new = jnp.maximum(m_prev, s_n)
        alpha = jnp.exp(m_prev - m_new)
        p_n = jnp.exp(s_n - m_new)
        l_fin = alpha * l_scr[...] + p_n
        acc = _expand_heads(alpha) * acc_scr[...] + _expand_heads(p_n) * vn_ref[...]
        o_ref[...] = acc / _expand_heads(l_fin)


def _samp_attn(page_flat, q3, bias3, kn3, vn3, expand, cache_k, cache_v, n_pages):
    bs = q3.shape[0]
    ng = n_pages // PG

    def page_spec(g):
        return pl.BlockSpec((None, PAGE_SIZE, ATT_W),
                            lambda b, j, pt: (pt[b * n_pages + j * PG + g], 0, 0))
    per_b = lambda a: pl.BlockSpec((None,) + a.shape[1:], lambda b, j, pt: (b, 0, 0))
    grid_spec = pltpu.PrefetchScalarGridSpec(
        num_scalar_prefetch=1,
        grid=(bs, ng),
        in_specs=[per_b(q3), per_b(bias3), per_b(kn3), per_b(vn3),
                  pl.BlockSpec(expand.shape, lambda b, j, pt: (0, 0))]
                 + [page_spec(g) for g in range(PG)] * 2,
        out_specs=pl.BlockSpec((None, 1, ATT_W), lambda b, j, pt: (b, 0, 0)),
        scratch_shapes=[pltpu.VMEM((ATT_W, LANES), BF),
                        pltpu.VMEM((1, LANES), F32),
                        pltpu.VMEM((1, LANES), F32),
                        pltpu.VMEM((1, ATT_W), F32)],
    )
    return pl.pallas_call(
        functools.partial(_samp_attn_kernel, n_pages=n_pages),
        grid_spec=grid_spec,
        out_shape=jax.ShapeDtypeStruct((bs, 1, ATT_W), F32),
        compiler_params=_params(("arbitrary", "arbitrary"), 48),
        name="samp_attn",
    )(page_flat, q3, bias3, kn3, vn3, expand, *([cache_k] * PG), *([cache_v] * PG))


def _rw_prep_kernel(pr_ref, prev_ref, pr8_ref, mu_ref, w0_ref, w2_ref, a0_ref, a2_ref, g2_ref,
                    kk_ref, ka_ref, rk_ref, bd_ref,
                    w_o, nkk_o, kka_o, km_o, r_o, v_o, g_o, bonus_o, *, shift_in_kernel, tiles_per_batch):
    x = pr_ref[...]
    if shift_in_kernel:
        first = (pl.program_id(0) % tiles_per_batch) == 0
        carry = jnp.where(first, prev_ref[...], pr8_ref[7:8, :])
        rolled = pltpu.roll(x, 1, axis=0)
        row = lax.broadcasted_iota(I32, (x.shape[0], 1), 0)
        prev = jnp.where(row == 0, carry, rolled)
    else:
        prev = prev_ref[...]
    xm = x + (prev - x) * mu_ref[...]
    r = xm[:, 0:RW_W]
    k = xm[:, RW_W:2 * RW_W]
    v = xm[:, 2 * RW_W:3 * RW_W]
    wa = xm[:, 3 * RW_W:3 * RW_W + W_LORA + A_LORA]
    gd = xm[:, 3 * RW_W + W_LORA + A_LORA:]
    wl = w0_ref[...] + jnp.dot(jnp.tanh(wa).astype(BF), w2_ref[...], preferred_element_type=F32)
    z = -wl
    softplus = jnp.maximum(z, 0.0) + jnp.log(1.0 + jnp.exp(-jnp.abs(z)))
    w_log = -softplus - 0.5
    a = _sigmoid(a0_ref[...] + jnp.dot(wa.astype(BF), a2_ref[...], preferred_element_type=F32))
    g = jnp.dot(_sigmoid(gd).astype(BF), g2_ref[...], preferred_element_type=F32)
    bd = bd_ref[...]
    kk = k * kk_ref[...]
    kk = kk * lax.rsqrt(jnp.maximum(_split_dot(kk * kk, bd), 1e-24))
    km = k * (1.0 + (a - 1.0) * ka_ref[...])
    w_o[...] = jnp.exp(-jnp.exp(w_log))
    nkk_o[...] = -kk
    kka_o[...] = kk * a
    km_o[...] = km
    r_o[...] = r
    v_o[...] = v
    g_o[...] = g
    bonus_o[...] = _split_dot(r * km * rk_ref[...], bd) * v


def _rw_prep(pr, prev, shift_in_kernel, tiles_per_batch, tm, mu, w0, w2p, a0, a2p, g2, k_k, k_a, r_k, bd512):
    n = pr.shape[0]
    row = lambda w: pl.BlockSpec((tm, w), lambda i: (i, 0))
    const = lambda a: pl.BlockSpec(a.shape, lambda i: (0,) * a.ndim)
    if shift_in_kernel:
        prev_spec = pl.BlockSpec((None, 1, RW_COLS), lambda i: (i // tiles_per_batch, 0, 0))
        pr8_spec = pl.BlockSpec((8, RW_COLS), lambda i: (jnp.maximum(i * (tm // 8) - 1, 0), 0))
    else:
        prev_spec = row(RW_COLS)
        pr8_spec = pl.BlockSpec((8, RW_COLS), lambda i: (0, 0))
    return pl.pallas_call(
        functools.partial(_rw_prep_kernel, shift_in_kernel=shift_in_kernel, tiles_per_batch=tiles_per_batch),
        grid=(n // tm,),
        in_specs=[row(RW_COLS), prev_spec, pr8_spec] + [const(a) for a in (mu, w0, w2p, a0, a2p, g2, k_k, k_a, r_k, bd512)],
        out_specs=[row(RW_W)] * 8,
        out_shape=[jax.ShapeDtypeStruct((n, RW_W), F32)] * 8,
        compiler_params=_params(("arbitrary",), 48),
        name="rw_prep",
    )(pr, prev, pr, mu, w0, w2p, a0, a2p, g2, k_k, k_a, r_k, bd512)


SCAN_BB = 8


def _wkv_scan_kernel(w_ref, nkk_ref, kka_ref, km_ref, r_ref, v_ref, s0_ref, y_ref, sfin_ref, s_scr):
    tb = pl.program_id(1)
    n_t = w_ref.shape[1]
    npair = RW_H // 2
    lane = lax.broadcasted_iota(I32, (1, LANES), 1)
    halves = (lane < RW_N, lane >= RW_N)
    ri = lax.broadcasted_iota(I32, (RW_N, LANES), 0)
    ci = lax.broadcasted_iota(I32, (RW_N, LANES), 1)
    eyes = (ci == ri, ci == ri + RW_N)

    @pl.when(tb == 0)
    def _():
        for b in range(SCAN_BB):
            for p in range(npair):
                packed = s0_ref[b, p]
                for half in range(2):
                    s_scr[(b * npair + p) * 2 + half] = jnp.where(halves[half], packed, 0.0)

    in_refs = (w_ref, nkk_ref, kka_ref, km_ref, r_ref, v_ref)
    n_sub = 8 if n_t % 8 == 0 else 1

    def pair_steps(b, p, base):
        cols = slice(p * LANES, (p + 1) * LANES)
        rows = [ref[b, pl.ds(base, n_sub), cols] for ref in in_refs]
        y_rows = []
        for j in range(n_sub):
            w, nkk, kka, km, r, v = [x[j:j + 1, :] for x in rows]
            y_pair = jnp.zeros((1, LANES), F32)
            for half in range(2):
                msk = halves[half]
                idx = (b * npair + p) * 2 + half
                s = s_scr[idx]
                sa = jnp.sum(s * jnp.where(msk, nkk, 0.0), axis=1, keepdims=True)
                vcol = jnp.sum(jnp.where(eyes[half], v, 0.0), axis=1, keepdims=True)
                s = (s * jnp.where(msk, w, 0.0) + sa * jnp.where(msk, kka, 0.0)
                     + vcol * jnp.where(msk, km, 0.0))
                s_scr[idx] = s
                ycol = jnp.sum(s * jnp.where(msk, r, 0.0), axis=1, keepdims=True)
                y_pair = y_pair + jnp.sum(jnp.where(eyes[half], ycol, 0.0), axis=0, keepdims=True)
            y_rows.append(y_pair)
        y_ref[b, pl.ds(base, n_sub), cols] = y_rows[0] if n_sub == 1 else jnp.concatenate(y_rows, axis=0)

    def time_group(u, c):
        base = u if isinstance(u, int) else pl.multiple_of(u * n_sub, n_sub)

        def batch_group(g, c2):
            for bb in range(2):
                for p in range(npair):
                    pair_steps(2 * g + bb, p, base)
            return c2
        return lax.fori_loop(0, SCAN_BB // 2, batch_group, c)
    if n_t == n_sub:
        time_group(0, 0)
    else:
        lax.fori_loop(0, n_t // n_sub, time_group, 0)

    @pl.when(tb == pl.num_programs(1) - 1)
    def _():
        for b in range(SCAN_BB):
            for p in range(npair):
                base = (b * npair + p) * 2
                sfin_ref[b, p] = s_scr[base] + s_scr[base + 1]


def _wkv_scan(w, nkk, kka, km, r, v, s0_packed, batch, seq, tb):
    shp = (batch, seq, RW_W)
    args = [a.reshape(shp) for a in (w, nkk, kka, km, r, v)]
    blk = pl.BlockSpec((SCAN_BB, tb, RW_W), lambda b, t: (b, t, 0))
    st = pl.BlockSpec((SCAN_BB, RW_H // 2, RW_N, LANES), lambda b, t: (b, 0, 0, 0))
    y, sfin = pl.pallas_call(
        _wkv_scan_kernel,
        grid=(batch // SCAN_BB, seq // tb),
        in_specs=[blk] * 6 + [st],
        out_specs=[blk, st],
        out_shape=[jax.ShapeDtypeStruct(shp, F32), jax.ShapeDtypeStruct(s0_packed.shape, F32)],
        scratch_shapes=[pltpu.VMEM((SCAN_BB * RW_H, RW_N, LANES), F32)],
        compiler_params=_params(("arbitrary", "arbitrary"), 48),
        name="wkv_scan",
    )(*args, s0_packed)
    return y.reshape(batch * seq, RW_W), sfin


def _pack_state(s):
    b = s.shape[0]
    return s.reshape(b, RW_H // 2, 2, RW_N, RW_N).transpose(0, 1, 3, 2, 4).reshape(b, RW_H // 2, RW_N, 2 * RW_N)


def _unpack_state(s):
    b = s.shape[0]
    return s.reshape(b, RW_H // 2, RW_N, 2, RW_N).transpose(0, 1, 3, 2, 4).reshape(b, RW_H, RW_N, RW_N)


def _out_ffn_kernel(x_ref, att_ref, y_ref, bonus_ref, g_ref, gate1_ref, sh2_ref, sc2_ref, gate2_ref,
                    lnw_ref, lnb_ref, n2g_ref, bd_ref, wo_ref, wg_ref, wu_ref, wd_ref, o_ref):
    bd = bd_ref[...]
    y = y_ref[...]
    mu = _split_dot(y, bd) * (1.0 / RW_N)
    d = y - mu
    var = _split_dot(d * d, bd) * (1.0 / RW_N)
    yn = d * lax.rsqrt(var + GN_EPS) * lnw_ref[...] + lnb_ref[...]
    rw = (yn + bonus_ref[...]) * g_ref[...]
    mix = (jnp.dot(att_ref[...].astype(BF), wo_ref[0:ATT_W, :], preferred_element_type=F32)
           + jnp.dot(rw.astype(BF), wo_ref[ATT_W:, :], preferred_element_type=F32))
    x1 = x_ref[...] + gate1_ref[...] * mix
    ms = jnp.mean(x1 * x1, axis=-1, keepdims=True)
    h2 = x1 * lax.rsqrt(ms + RMS_EPS) * n2g_ref[...]
    hb = (h2 * (1.0 + sc2_ref[...]) + sh2_ref[...]).astype(BF)
    a = jnp.dot(hb, wg_ref[...], preferred_element_type=F32)
    u = jnp.dot(hb, wu_ref[...], preferred_element_type=F32)
    f = (a * _sigmoid(a)) * u
    ffn = jnp.dot(f.astype(BF), wd_ref[...], preferred_element_type=F32)
    o_ref[...] = x1 + gate2_ref[...] * ffn


def _out_ffn(x2d, att, y, bonus, g, mod, per_batch, tiles_per_batch, tm,
             ln_w, ln_b, norm2_g, bd512, wo, wg, wu, wd):
    n, d = x2d.shape
    row = lambda a: pl.BlockSpec((tm, a.shape[1]), lambda i: (i, 0))
    const = lambda a: pl.BlockSpec(a.shape, lambda i: (0,) * a.ndim, pipeline_mode=pl.Buffered(1))
    ms = lambda c: _mod_spec(per_batch, c, tiles_per_batch, tm, d)
    return pl.pallas_call(
        _out_ffn_kernel,
        grid=(n // tm,),
        in_specs=[row(x2d), row(att), row(y), row(bonus), row(g), ms(2), ms(3), ms(4), ms(5)]
                 + [const(a) for a in (ln_w, ln_b, norm2_g, bd512, wo, wg, wu, wd)],
        out_specs=pl.BlockSpec((tm, d), lambda i: (i, 0)),
        out_shape=jax.ShapeDtypeStruct((n, d), F32),
        compiler_params=_params(("arbitrary",), 56),
        name="out_ffn",
    )(x2d, att, y, bonus, g, mod, mod, mod, mod, ln_w, ln_b, norm2_g, bd512, wo, wg, wu, wd)


def _block_diag_ones(width, seg):
    i = jnp.arange(width)
    return (i[:, None] // seg == i[None, :] // seg).astype(BF)


def kernel(x_prompt, x_sample, cache_k, cache_v, cache_idx_k, state_wkv, state_shift, page_table,
           c_prompt, c_sample, norm1_g, norm2_g, w_ada, b_ada, w_in, q_norm_g, k_norm_g,
           rw_mu, rw_w0, rw_w2, rw_a0, rw_a2, rw_g2, rw_k_k, rw_k_a, rw_r_k, rw_ln_w, rw_ln_b,
           w_out, w_ffn_gate, w_ffn_up, w_ffn_down):
    bp, seq, d = x_prompt.shape
    bs, dec_seq, _ = x_sample.shape
    assert dec_seq == 1 and w_ada.shape[0] == 1
    n_pages = page_table.shape[1]
    past = n_pages * PAGE_SIZE
    n_pool = cache_k.shape[1]
    np_tok = bp * seq

    wi = w_in[0]
    w_in_p = jnp.concatenate([wi[:, :ATT_COLS], jnp.zeros((d, C_PR - ATT_COLS), F32), wi[:, ATT_COLS:]],
                             axis=1).astype(BF)
    tile_h = lambda g: jnp.tile(g.reshape(1, HEAD_DIM), (1, ATT_H))
    qg, kg = tile_h(q_norm_g[0]), tile_h(k_norm_g[0])
    bd512 = _block_diag_ones(ATT_W, HEAD_DIM)
    row = lambda a: a.reshape(1, -1)
    w2p = jnp.concatenate([rw_w2[0], jnp.zeros((A_LORA, RW_W), F32)], axis=0).astype(BF)
    a2p = jnp.concatenate([jnp.zeros((W_LORA, RW_W), F32), rw_a2[0]], axis=0).astype(BF)
    g2b = rw_g2[0].astype(BF)
    wo, wg, wu, wd = (w_out[0].astype(BF), w_ffn_gate[0].astype(BF), w_ffn_up[0].astype(BF),
                      w_ffn_down[0].astype(BF))

    mod = _adaln(jnp.concatenate([c_sample, c_prompt], axis=0), w_ada[0], b_ada[0])
    mod_s = mod[:bs]
    mod_p = mod[bs:].reshape(bp, 1, 6 * d)

    def rwkv_and_out(x2d, att, pr, prev, shift_in_kernel, s0_packed, batch, t_len, tb, mod_x, per_batch,
                     tiles_per_batch, tm):
        w, nkk, kka, km, r, v, g, bonus = _rw_prep(
            pr, prev, shift_in_kernel, tiles_per_batch, tm, row(rw_mu[0]), row(rw_w0[0]), w2p, row(rw_a0[0]),
            a2p, g2b, row(rw_k_k[0]), row(rw_k_a[0]), row(rw_r_k[0]), bd512)
        y, sfin = _wkv_scan(w, nkk, kka, km, r, v, s0_packed, batch, t_len, tb)
        out = _out_ffn(x2d, att, y, bonus, g, mod_x, per_batch, tiles_per_batch, tm,
                       row(rw_ln_w[0]), row(rw_ln_b[0]), row(norm2_g[0]), bd512, wo, wg, wu, wd)
        return out, _unpack_state(sfin)

    tm = 256
    tpb = seq // tm
    xp = x_prompt.reshape(np_tok, d)
    (q_bf, k_p, v_p, ke, ko, ve, vo, qi_bf, kiw_p, kie, kio, pr_p) = _inproj(
        xp, mod_p, True, tpb, tm, row(norm1_g[0]), w_in_p, qg, kg, bd512)
    att_p = _attn_prompt(q_bf, qi_bf, kiw_p, ke, ko, ve, vo, kie, kio, bp, seq)
    y_p, wkv_p = rwkv_and_out(xp, att_p, pr_p, jnp.zeros((bp, 1, RW_COLS), F32), True,
                              jnp.zeros((bp, RW_H // 2, RW_N, 2 * RW_N), F32), bp, seq, 128, mod_p, True, tpb, tm)

    xs = x_sample.reshape(bs, d)
    (q_s, k_s, v_s, _, _, _, _, qi_s, kiw_s, _, _, pr_s) = _inproj(
        xs, mod_s, False, 1, bs, row(norm1_g[0]), w_in_p, qg, kg, bd512)
    page_flat = page_table.reshape(-1).astype(I32)
    scores = _samp_scores(page_flat, qi_s.reshape(bs, IDX_H, IDX_D),
                          kiw_s[:, IDX_D:IDX_D + IDX_H].reshape(bs, IDX_H, 1), cache_idx_k[0], n_pages)
    ii = jnp.arange(LANES)[:, None]
    jj = jnp.arange(IDX_H * IDX_D)[None, :]
    rep = ((ii < IDX_D) & (jj % IDX_D == ii)).astype(BF)
    hsum = (jnp.arange(IDX_H * IDX_D)[:, None] // IDX_D == jnp.arange(LANES)[None, :]).astype(BF)
    bias = _samp_select(scores.reshape(bs, past), qi_s, kiw_s, rep, hsum, past)
    expand = (jnp.arange(LANES)[:, None] == jnp.arange(ATT_W)[None, :] // HEAD_DIM).astype(BF)
    att_s = _samp_attn(page_flat, q_s.reshape(bs, 1, ATT_W), bias.reshape(bs, n_pages + 1, LANES),
                       k_s.reshape(bs, 1, ATT_W), v_s.reshape(bs, 1, ATT_W), expand,
                       cache_k[0].reshape(n_pool, PAGE_SIZE, ATT_W), cache_v[0].reshape(n_pool, PAGE_SIZE, ATT_W),
                       n_pages).reshape(bs, ATT_W)
    y_s, wkv_s = rwkv_and_out(xs, att_s, pr_s, state_shift[0], False, _pack_state(state_wkv[0]),
                              bs, 1, 1, mod_s, False, 1, bs)

    hd = (ATT_H, HEAD_DIM)
    return (y_p.reshape(bp, seq, d), y_s.reshape(bs, 1, d),
            k_p.reshape((1, bp, seq) + hd), v_p.reshape((1, bp, seq) + hd),
            kiw_p[:, :IDX_D].reshape(1, bp, seq, IDX_D),
            wkv_p[None], pr_p.reshape(bp, seq, RW_COLS)[:, -1][None],
            k_s.reshape((1, bs, 1) + hd), v_s.reshape((1, bs, 1) + hd),
            kiw_s[:, :IDX_D].reshape(1, bs, 1, IDX_D),
            wkv_s[None], pr_s[None])
```

```python
import functools

import jax
import jax.numpy as jnp
from jax import lax
from jax.experimental import pallas as pl
from jax.experimental.pallas import tpu as pltpu

F32 = jnp.float32
BF = jnp.bfloat16
I32 = jnp.int32

LANES = 128
HEAD_DIM = 64
ATT_H = 8
ATT_W = ATT_H * HEAD_DIM
IDX_H = 16
IDX_D = 64
IDX_W_SCALE = (IDX_H ** -0.5) * (IDX_D ** -0.5)
RW_N = 64
RW_H = 8
RW_W = RW_H * RW_N
W_LORA = 64
A_LORA = 64
G_LORA = 128
TOPK_MAX = 256
PAGE_SIZE = 128
RMS_EPS = 1e-6
GN_EPS = 64e-5
NEG_BIG = -1e30
INT_MIN = -2147483648

C_Q, C_K, C_V, C_QI = 0, ATT_W, 2 * ATT_W, 3 * ATT_W
C_KIW = C_QI + IDX_H * IDX_D
C_PR = C_KIW + LANES
ATT_COLS = 3 * ATT_W + IDX_H * IDX_D + IDX_D + IDX_H
RW_COLS = 3 * RW_W + W_LORA + A_LORA + G_LORA
IN_COLS_PAD = C_PR + RW_COLS

NT_DIMS = (((1,), (1,)), ((), ()))


def _params(sem, vmem_mb=48):
    return pltpu.CompilerParams(dimension_semantics=sem, vmem_limit_bytes=vmem_mb * 1024 * 1024)


def _split_dot(z, ones_bd):
    zh = z.astype(BF)
    zl = (z - zh.astype(F32)).astype(BF)
    return (jnp.dot(zh, ones_bd, preferred_element_type=F32)
            + jnp.dot(zl, ones_bd, preferred_element_type=F32))


def _sigmoid(x):
    return 1.0 / (1.0 + jnp.exp(-x))


def _mod_kernel(c_ref, w_ref, b_ref, o_ref):
    c = c_ref[...]
    s = c * _sigmoid(c)
    o_ref[...] = jnp.dot(s.astype(BF), w_ref[...].astype(BF), preferred_element_type=F32) + b_ref[...]


def _adaln(c_all, w_ada, b_ada):
    n, d = c_all.shape
    d6 = w_ada.shape[1]
    return pl.pallas_call(
        _mod_kernel,
        grid=(d6 // d,),
        in_specs=[pl.BlockSpec((n, d), lambda j: (0, 0)),
                  pl.BlockSpec((d, d), lambda j: (0, j)),
                  pl.BlockSpec((1, d), lambda j: (0, j))],
        out_specs=pl.BlockSpec((n, d), lambda j: (0, j)),
        out_shape=jax.ShapeDtypeStruct((n, d6), F32),
        compiler_params=_params(("arbitrary",)),
        name="adaln_mod",
    )(c_all, w_ada, b_ada.reshape(1, d6))


def _mod_spec(per_batch, chunk, tiles_per_batch, tm, d):
    if per_batch:
        return pl.BlockSpec((None, 1, d), lambda i: (i // tiles_per_batch, 0, chunk))
    return pl.BlockSpec((tm, d), lambda i: (i, chunk))


def _inproj_kernel(x_ref, sh_ref, sc_ref, g1_ref, w_ref, qg_ref, kg_ref, bd_ref,
                   q_o, k_o, v_o, ke_o, ko_o, ve_o, vo_o, qi_o, kiw_o, kie_o, kio_o, pr_o):
    x = x_ref[...]
    ms = jnp.mean(x * x, axis=-1, keepdims=True)
    h = x * lax.rsqrt(ms + RMS_EPS) * g1_ref[...]
    h = h * (1.0 + sc_ref[...]) + sh_ref[...]
    hb = h.astype(BF)

    def mm(lo, hi):
        return jnp.dot(hb, w_ref[:, lo:hi], preferred_element_type=F32)

    bd = bd_ref[...]

    def headnorm(z, g):
        ss = _split_dot(z * z, bd)
        return z * lax.rsqrt(ss * (1.0 / HEAD_DIM) + RMS_EPS) * g

    lane = lax.broadcasted_iota(I32, (1, ATT_W), 1)
    even = (lane & HEAD_DIM) == 0

    q = headnorm(mm(C_Q, C_K), qg_ref[...]) * (HEAD_DIM ** -0.5)
    q_o[...] = q.astype(BF)
    k = headnorm(mm(C_K, C_V), kg_ref[...])
    k_o[...] = k
    ke_o[...] = jnp.where(even, k, 0.0).astype(BF)
    ko_o[...] = jnp.where(even, 0.0, k).astype(BF)
    v = mm(C_V, C_QI)
    v_o[...] = v
    ve_o[...] = jnp.where(even, v, 0.0).astype(BF)
    vo_o[...] = jnp.where(even, 0.0, v).astype(BF)
    qi_o[...] = mm(C_QI, C_KIW).astype(BF)

    kw = mm(C_KIW, C_PR)
    lane1 = lax.broadcasted_iota(I32, (1, LANES), 1)
    is_key = lane1 < IDX_D
    ss = jnp.sum(jnp.where(is_key, kw * kw, 0.0), axis=-1, keepdims=True)
    kin = kw * lax.rsqrt(ss * (1.0 / IDX_D) + RMS_EPS)
    kiw_o[...] = jnp.where(is_key, kin, kw * IDX_W_SCALE)
    ki_only = jnp.where(is_key, kin, 0.0)
    kie_o[...] = ki_only.astype(BF)
    kio_o[...] = pltpu.roll(ki_only, IDX_D, axis=1).astype(BF)
    pr_o[...] = mm(C_PR, IN_COLS_PAD)


def _inproj(x2d, mod, per_batch, tiles_per_batch, tm, norm1_g, w_in_p, qg, kg, bd512):
    n, d = x2d.shape
    row = lambda w: pl.BlockSpec((tm, w), lambda i: (i, 0))
    const = lambda a: pl.BlockSpec(a.shape, lambda i: (0,) * a.ndim)
    outs = [(ATT_W, BF), (ATT_W, F32), (ATT_W, F32), (ATT_W, BF), (ATT_W, BF), (ATT_W, BF), (ATT_W, BF),
            (IDX_H * IDX_D, BF), (LANES, F32), (LANES, BF), (LANES, BF), (RW_COLS, F32)]
    return pl.pallas_call(
        _inproj_kernel,
        grid=(n // tm,),
        in_specs=[row(d),
                  _mod_spec(per_batch, 0, tiles_per_batch, tm, d),
                  _mod_spec(per_batch, 1, tiles_per_batch, tm, d),
                  const(norm1_g), const(w_in_p), const(qg), const(kg), const(bd512)],
        out_specs=[row(w) for w, _ in outs],
        out_shape=[jax.ShapeDtypeStruct((n, w), dt) for w, dt in outs],
        compiler_params=_params(("arbitrary",), 56),
        name="inproj",
    )(x2d, mod, mod, norm1_g, w_in_p, qg, kg, bd512)


def _sortable_key(s):
    s = jnp.where(s == 0.0, 0.0, s)
    bits = lax.bitcast_convert_type(s, I32)
    return jnp.where(bits < 0, bits ^ 0x7FFFFFFF, bits)


def _bisect_bits(count_ge, n_sel, shape):
    def body(it, thr_u):
        cand_u = thr_u | lax.shift_left(jnp.int32(1), 31 - it)
        cnt = count_ge(cand_u ^ INT_MIN)
        return jnp.where(cnt >= n_sel, cand_u, thr_u)
    thr_u = lax.fori_loop(0, 32, body, jnp.zeros(shape, I32))
    return thr_u ^ INT_MIN


QB = 256


def _attn_prompt_kernel(q_ref, qi_ref, kiwq_ref, ke_ref, ko_ref, ve_ref, vo_ref, kie_ref, kio_ref, o_ref,
                        key_scr, sc_scr, m_scr, l_scr, acc_scr, *, n_sel):
    i = pl.program_id(1)
    nch = i + 1
    wt = kiwq_ref[...].T
    t_glob = i * QB + lax.broadcasted_iota(I32, (QB, QB), 1)
    s_loc = lax.broadcasted_iota(I32, (QB, QB), 0)

    def score_chunk(j, c):
        off = pl.multiple_of(j * QB, QB)
        ke = kie_ref[pl.ds(off, QB), :]
        ko = kio_ref[pl.ds(off, QB), :]
        for hp in range(IDX_H // 2):
            qt = qi_ref[:, hp * LANES:(hp + 1) * LANES]
            d0 = lax.dot_general(ke, qt, NT_DIMS, preferred_element_type=F32)
            d1 = lax.dot_general(ko, qt, NT_DIMS, preferred_element_type=F32)
            r0 = IDX_D + 2 * hp
            contrib = (wt[r0:r0 + 1, :] * jnp.maximum(d0, 0.0)
                       + wt[r0 + 1:r0 + 2, :] * jnp.maximum(d1, 0.0))
            if hp == 0:
                sc_scr[...] = contrib
            else:
                sc_scr[...] += contrib
        allowed = (off + s_loc) <= t_glob
        key_scr[pl.ds(off, QB), :] = _sortable_key(jnp.where(allowed, sc_scr[...], -jnp.inf))
        return c
    lax.fori_loop(0, nch, score_chunk, 0)

    def count_ge(cand):
        def body(j, acc):
            off = pl.multiple_of(j * QB, QB)
            ind = jnp.where(key_scr[pl.ds(off, QB), :] >= cand, 1.0, 0.0)
            return acc + ind.reshape(QB // 8, 8, QB).sum(axis=0)
        acc = lax.fori_loop(0, nch, body, jnp.zeros((8, QB), F32))
        return acc.sum(axis=0, keepdims=True)

    thr = _bisect_bits(count_ge, float(n_sel), (1, QB))

    m_scr[...] = jnp.full(m_scr.shape, NEG_BIG, F32)
    l_scr[...] = jnp.zeros(l_scr.shape, F32)
    acc_scr[...] = jnp.zeros(acc_scr.shape, F32)

    def attn_chunk(j, c):
        off = pl.multiple_of(j * QB, QB)
        allowed = (off + s_loc) <= t_glob
        sel_t = (key_scr[pl.ds(off, QB), :] >= thr) & allowed
        bias = jnp.where(sel_t, 0.0, NEG_BIG).T
        for h in range(ATT_H):
            p, par = divmod(h, 2)
            cols = slice(p * LANES, (p + 1) * LANES)
            k_src, v_src = (ke_ref, ve_ref) if par == 0 else (ko_ref, vo_ref)
            s = lax.dot_general(q_ref[:, cols], k_src[pl.ds(off, QB), cols], NT_DIMS,
                                preferred_element_type=F32) + bias
            m_prev = m_scr[h]
            m_new = jnp.maximum(m_prev, jnp.max(s, axis=1, keepdims=True))
            alpha = jnp.exp(m_prev - m_new)
            pexp = jnp.exp(s - m_new)
            l_scr[h] = alpha * l_scr[h] + jnp.sum(pexp, axis=1, keepdims=True)
            acc_scr[h] = alpha * acc_scr[h] + jnp.dot(pexp.astype(BF), v_src[pl.ds(off, QB), cols],
                                                      preferred_element_type=F32)
            m_scr[h] = m_new
        return c
    lax.fori_loop(0, nch, attn_chunk, 0)

    for p in range(ATT_H // 2):
        o = acc_scr[2 * p] / l_scr[2 * p] + acc_scr[2 * p + 1] / l_scr[2 * p + 1]
        o_ref[:, p * LANES:(p + 1) * LANES] = o.astype(o_ref.dtype)


def _attn_prompt(q_bf, qi_bf, kiw, ke, ko, ve, vo, kie, kio, batch, seq):
    n = batch * seq
    nq = seq // QB
    n_sel = min(TOPK_MAX, seq // 4)
    qrow = lambda w: pl.BlockSpec((QB, w), lambda b, i: (b * nq + i, 0))
    kv = lambda w: pl.BlockSpec((seq, w), lambda b, i: (b, 0))
    return pl.pallas_call(
        functools.partial(_attn_prompt_kernel, n_sel=n_sel),
        grid=(batch, nq),
        in_specs=[qrow(ATT_W), qrow(IDX_H * IDX_D), qrow(LANES),
                  kv(ATT_W), kv(ATT_W), kv(ATT_W), kv(ATT_W), kv(LANES), kv(LANES)],
        out_specs=qrow(ATT_W),
        out_shape=jax.ShapeDtypeStruct((n, ATT_W), BF),
        scratch_shapes=[pltpu.VMEM((seq, QB), I32),
                        pltpu.VMEM((QB, QB), F32),
                        pltpu.VMEM((ATT_H, QB, 1), F32),
                        pltpu.VMEM((ATT_H, QB, 1), F32),
                        pltpu.VMEM((ATT_H, QB, LANES), F32)],
        compiler_params=_params(("arbitrary", "arbitrary"), 48),
        name="attn_prompt",
    )(q_bf, qi_bf, kiw, ke, ko, ve, vo, kie, kio)


PG = 8


def _samp_score_kernel(pt_ref, qi_ref, w_ref, *rest):
    page_refs, o_ref = rest[:PG], rest[PG]
    qi = qi_ref[...]
    w = w_ref[...]
    for g in range(PG):
        kp = page_refs[g][...].astype(BF)
        d = lax.dot_general(qi, kp, NT_DIMS, preferred_element_type=F32)
        o_ref[:, g * PAGE_SIZE:(g + 1) * PAGE_SIZE] = jnp.sum(w * jnp.maximum(d, 0.0), axis=0, keepdims=True)


def _samp_scores(page_flat, qi3, w3, cache_ik, n_pages):
    bs = qi3.shape[0]
    ng = n_pages // PG

    def page_spec(g):
        return pl.BlockSpec((None, PAGE_SIZE, IDX_D),
                            lambda b, j, pt: (pt[b * n_pages + j * PG + g], 0, 0))
    grid_spec = pltpu.PrefetchScalarGridSpec(
        num_scalar_prefetch=1,
        grid=(bs, ng),
        in_specs=[pl.BlockSpec((None, IDX_H, IDX_D), lambda b, j, pt: (b, 0, 0)),
                  pl.BlockSpec((None, IDX_H, 1), lambda b, j, pt: (b, 0, 0))]
                 + [page_spec(g) for g in range(PG)],
        out_specs=pl.BlockSpec((None, None, 1, PG * PAGE_SIZE), lambda b, j, pt: (b, j, 0, 0)),
    )
    return pl.pallas_call(
        _samp_score_kernel,
        grid_spec=grid_spec,
        out_shape=jax.ShapeDtypeStruct((bs, ng, 1, PG * PAGE_SIZE), F32),
        compiler_params=_params(("arbitrary", "arbitrary"), 32),
        name="samp_scores",
    )(page_flat, qi3, w3, *([cache_ik] * PG))


def _samp_select_kernel(sc_ref, qi_ref, kiw_ref, rep_ref, hsum_ref, o_ref, *, n_sel, past):
    kiw = kiw_ref[...]
    lane = lax.broadcasted_iota(I32, kiw.shape, 1)
    ki = jnp.where(lane < IDX_D, kiw, 0.0).astype(BF)
    ki_rep = jnp.dot(ki, rep_ref[...], preferred_element_type=F32)
    prod = qi_ref[...].astype(F32) * ki_rep
    d_new = _split_dot(prod, hsum_ref[...])
    w_new = pltpu.roll(kiw, LANES - IDX_D, axis=1)
    s_new = jnp.sum(jnp.where(lane < IDX_H, w_new * jnp.maximum(d_new, 0.0), 0.0), axis=1, keepdims=True)
    new_chunk = jnp.where(lane == 0, s_new, -jnp.inf)
    key = _sortable_key(jnp.concatenate([sc_ref[...], new_chunk], axis=1))
    pos = lax.broadcasted_iota(I32, key.shape, 1)

    def count_ge(cand):
        return jnp.sum(jnp.where(key >= cand, 1.0, 0.0), axis=1, keepdims=True)
    thr = _bisect_bits(count_ge, float(n_sel), (key.shape[0], 1))

    gt = key > thr
    eq = key == thr
    need = float(n_sel) - jnp.sum(jnp.where(gt, 1.0, 0.0), axis=1, keepdims=True)
    nbits = max(1, int(key.shape[1]).bit_length())

    def cut_body(it, cut):
        cand = cut | lax.shift_left(jnp.int32(1), nbits - 1 - it)
        cnt = jnp.sum(jnp.where(eq & (pos < cand), 1.0, 0.0), axis=1, keepdims=True)
        return jnp.where(cnt <= need, cand, cut)
    cut = lax.fori_loop(0, nbits, cut_body, jnp.zeros((key.shape[0], 1), I32))
    sel = (gt | (eq & (pos < cut))) & (pos <= past)
    o_ref[...] = jnp.where(sel, 0.0, NEG_BIG)


def _samp_select(scores, qi_s, kiw_s, rep, hsum, past):
    bs = scores.shape[0]
    n_sel = min(TOPK_MAX, (past + 1) // 4)
    full = lambda a: pl.BlockSpec(a.shape, lambda i: (0,) * a.ndim)
    return pl.pallas_call(
        functools.partial(_samp_select_kernel, n_sel=n_sel, past=past),
        grid=(1,),
        in_specs=[full(scores), full(qi_s), full(kiw_s), full(rep), full(hsum)],
        out_specs=pl.BlockSpec((bs, past + LANES), lambda i: (0, 0)),
        out_shape=jax.ShapeDtypeStruct((bs, past + LANES), F32),
        compiler_params=_params(("arbitrary",), 32),
        name="samp_select",
    )(scores, qi_s, kiw_s, rep, hsum)


def _row_to_col(row, eye):
    return jnp.sum(jnp.where(eye, row, 0.0), axis=1, keepdims=True)


def _expand_heads(x):
    lane = lax.broadcasted_iota(I32, (1, LANES), 1)
    tiles = []
    for p in range(ATT_H // 2):
        a = jnp.sum(jnp.where(lane == 2 * p, x, 0.0), axis=1, keepdims=True)
        b = jnp.sum(jnp.where(lane == 2 * p + 1, x, 0.0), axis=1, keepdims=True)
        tiles.append(jnp.where(lane < HEAD_DIM, a, b))
    return jnp.concatenate(tiles, axis=1)


def _samp_attn_kernel(pt_ref, q_ref, bias_ref, kn_ref, vn_ref, ex_ref, *rest,
                      n_pages):
    k_refs, v_refs = rest[:PG], rest[PG:2 * PG]
    o_ref = rest[2 * PG]
    qbd_scr, m_scr, l_scr, acc_scr = rest[2 * PG + 1:]
    j = pl.program_id(1)
    eye = (lax.broadcasted_iota(I32, (LANES, LANES), 0) == lax.broadcasted_iota(I32, (LANES, LANES), 1))

    @pl.when(j == 0)
    def _():
        r = lax.broadcasted_iota(I32, (LANES, LANES), 0)
        c = lax.broadcasted_iota(I32, (LANES, LANES), 1)
        for t in range(ATT_W // LANES):
            col = _row_to_col(q_ref[:, t * LANES:(t + 1) * LANES].astype(F32), eye)
            qbd_scr[t * LANES:(t + 1) * LANES, :] = jnp.where((t * LANES + r) // HEAD_DIM == c, col, 0.0).astype(BF)
        m_scr[...] = jnp.full(m_scr.shape, NEG_BIG, F32)
        l_scr[...] = jnp.zeros(l_scr.shape, F32)
        acc_scr[...] = jnp.zeros(acc_scr.shape, F32)

    qbd = qbd_scr[...]
    ex = ex_ref[...]
    s_list = []
    for g in range(PG):
        kp = k_refs[g][...].astype(BF)
        s = jnp.dot(kp, qbd, preferred_element_type=F32)
        brow = bias_ref[pl.ds(j * PG + g, 1), :]
        s_list.append(s + _row_to_col(brow, eye))
    m_prev = m_scr[...]
    m_new = m_prev
    for s in s_list:
        m_new = jnp.maximum(m_new, jnp.max(s, axis=0, keepdims=True))
    alpha = jnp.exp(m_prev - m_new)
    l_new = alpha * l_scr[...]
    part = jnp.zeros((8, ATT_W), F32)
    for g in range(PG):
        pexp = jnp.exp(s_list[g] - m_new)
        l_new = l_new + jnp.sum(pexp, axis=0, keepdims=True)
        pe = jnp.dot(pexp.astype(BF), ex, preferred_element_type=F32)
        part = part + (pe * v_refs[g][...]).reshape(PAGE_SIZE // 8, 8, ATT_W).sum(axis=0)
    acc_scr[...] = _expand_heads(alpha) * acc_scr[...] + part.sum(axis=0, keepdims=True)
    l_scr[...] = l_new
    m_scr[...] = m_new

    @pl.when(j == pl.num_programs(1) - 1)
    def _():
        kn = jnp.broadcast_to(kn_ref[...], (8, ATT_W)).astype(BF)
        s_n = jnp.dot(kn, qbd, preferred_element_type=F32)[0:1, :]
        brow = bias_ref[pl.ds(n_pages, 1), :]
        lane = lax.broadcasted_iota(I32, (1, LANES), 1)
        b_n = jnp.sum(jnp.where(lane == 0, brow, 0.0), axis=1, keepdims=True)
        s_n = s_n + b_n
        m_prev = m_scr[...]
        m_new = jnp.maximum(m_prev, s_n)
        alpha = jnp.exp(m_prev - m_new)
        p_n = jnp.exp(s_n - m_new)
        l_fin = alpha * l_scr[...] + p_n
        acc = _expand_heads(alpha) * acc_scr[...] + _expand_heads(p_n) * vn_ref[...]
        o_ref[...] = acc / _expand_heads(l_fin)


def _samp_attn(page_flat, q3, bias3, kn3, vn3, expand, cache_k, cache_v, n_pages):
    bs = q3.shape[0]
    ng = n_pages // PG

    def page_spec(g):
        return pl.BlockSpec((None, PAGE_SIZE, ATT_W),
                            lambda b, j, pt: (pt[b * n_pages + j * PG + g], 0, 0))
    per_b = lambda a: pl.BlockSpec((None,) + a.shape[1:], lambda b, j, pt: (b, 0, 0))
    grid_spec = pltpu.PrefetchScalarGridSpec(
        num_scalar_prefetch=1,
        grid=(bs, ng),
        in_specs=[per_b(q3), per_b(bias3), per_b(kn3), per_b(vn3),
                  pl.BlockSpec(expand.shape, lambda b, j, pt: (0, 0))]
                 + [page_spec(g) for g in range(PG)] * 2,
        out_specs=pl.BlockSpec((None, 1, ATT_W), lambda b, j, pt: (b, 0, 0)),
        scratch_shapes=[pltpu.VMEM((ATT_W, LANES), BF),
                        pltpu.VMEM((1, LANES), F32),
                        pltpu.VMEM((1, LANES), F32),
                        pltpu.VMEM((1, ATT_W), F32)],
    )
    return pl.pallas_call(
        functools.partial(_samp_attn_kernel, n_pages=n_pages),
        grid_spec=grid_spec,
        out_shape=jax.ShapeDtypeStruct((bs, 1, ATT_W), F32),
        compiler_params=_params(("arbitrary", "arbitrary"), 48),
        name="samp_attn",
    )(page_flat, q3, bias3, kn3, vn3, expand, *([cache_k] * PG), *([cache_v] * PG))


def _rw_prep_kernel(pr_ref, prev_ref, pr8_ref, mu_ref, w0_ref, w2_ref, a0_ref, a2_ref, g2_ref,
                    kk_ref, ka_ref, rk_ref, bd_ref,
                    w_o, nkk_o, kka_o, km_o, r_o, v_o, g_o, bonus_o, *, shift_in_kernel, tiles_per_batch):
    x = pr_ref[...]
    if shift_in_kernel:
        first = (pl.program_id(0) % tiles_per_batch) == 0
        carry = jnp.where(first, prev_ref[...], pr8_ref[7:8, :])
        rolled = pltpu.roll(x, 1, axis=0)
        row = lax.broadcasted_iota(I32, (x.shape[0], 1), 0)
        prev = jnp.where(row == 0, carry, rolled)
    else:
        prev = prev_ref[...]
    xm = x + (prev - x) * mu_ref[...]
    r = xm[:, 0:RW_W]
    k = xm[:, RW_W:2 * RW_W]
    v = xm[:, 2 * RW_W:3 * RW_W]
    wa = xm[:, 3 * RW_W:3 * RW_W + W_LORA + A_LORA]
    gd = xm[:, 3 * RW_W + W_LORA + A_LORA:]
    wl = w0_ref[...] + jnp.dot(jnp.tanh(wa).astype(BF), w2_ref[...], preferred_element_type=F32)
    z = -wl
    softplus = jnp.maximum(z, 0.0) + jnp.log(1.0 + jnp.exp(-jnp.abs(z)))
    w_log = -softplus - 0.5
    a = _sigmoid(a0_ref[...] + jnp.dot(wa.astype(BF), a2_ref[...], preferred_element_type=F32))
    g = jnp.dot(_sigmoid(gd).astype(BF), g2_ref[...], preferred_element_type=F32)
    bd = bd_ref[...]
    kk = k * kk_ref[...]
    kk = kk * lax.rsqrt(jnp.maximum(_split_dot(kk * kk, bd), 1e-24))
    km = k * (1.0 + (a - 1.0) * ka_ref[...])
    w_o[...] = jnp.exp(-jnp.exp(w_log))
    nkk_o[...] = -kk
    kka_o[...] = kk * a
    km_o[...] = km
    r_o[...] = r
    v_o[...] = v
    g_o[...] = g
    bonus_o[...] = _split_dot(r * km * rk_ref[...], bd) * v


def _rw_prep(pr, prev, shift_in_kernel, tiles_per_batch, tm, mu, w0, w2p, a0, a2p, g2, k_k, k_a, r_k, bd512):
    n = pr.shape[0]
    row = lambda w: pl.BlockSpec((tm, w), lambda i: (i, 0))
    const = lambda a: pl.BlockSpec(a.shape, lambda i: (0,) * a.ndim)
    if shift_in_kernel:
        prev_spec = pl.BlockSpec((None, 1, RW_COLS), lambda i: (i // tiles_per_batch, 0, 0))
        pr8_spec = pl.BlockSpec((8, RW_COLS), lambda i: (jnp.maximum(i * (tm // 8) - 1, 0), 0))
    else:
        prev_spec = row(RW_COLS)
        pr8_spec = pl.BlockSpec((8, RW_COLS), lambda i: (0, 0))
    return pl.pallas_call(
        functools.partial(_rw_prep_kernel, shift_in_kernel=shift_in_kernel, tiles_per_batch=tiles_per_batch),
        grid=(n // tm,),
        in_specs=[row(RW_COLS), prev_spec, pr8_spec] + [const(a) for a in (mu, w0, w2p, a0, a2p, g2, k_k, k_a, r_k, bd512)],
        out_specs=[row(RW_W)] * 8,
        out_shape=[jax.ShapeDtypeStruct((n, RW_W), F32)] * 8,
        compiler_params=_params(("arbitrary",), 48),
        name="rw_prep",
    )(pr, prev, pr, mu, w0, w2p, a0, a2p, g2, k_k, k_a, r_k, bd512)


SCAN_G = 8
SCAN_IH = RW_N // 2
SUBL = 8


def _sublane_allsum(x):
    x = x + pltpu.roll(x, 4, axis=0)
    x = x + pltpu.roll(x, 2, axis=0)
    return x + pltpu.roll(x, 1, axis=0)


def _tree_sum(xs):
    while len(xs) > 1:
        xs = [xs[i] + xs[i + 1] for i in range(0, len(xs) - 1, 2)] + ([xs[-1]] if len(xs) % 2 else [])
    return xs[0]


def _wkv_scan_kernel(w_ref, nkk_ref, kka_ref, km_ref, r_ref, v_ref, s0_ref, y_ref, sfin_ref, s_scr):
    tb = pl.program_id(1)
    n_t = w_ref.shape[0]
    njb = RW_N // SUBL
    sub = lax.broadcasted_iota(I32, (SUBL, LANES), 0)

    @pl.when(tb == 0)
    def _():
        s_scr[...] = s0_ref[...]

    def step(t, c):
        def rows(ref):
            return [ref[t, jb * SUBL:(jb + 1) * SUBL, :] for jb in range(njb)]
        w, nkk, kka, km, r = rows(w_ref), rows(nkk_ref), rows(kka_ref), rows(km_ref), rows(r_ref)
        for ig in range(SCAN_IH // SUBL):
            vt = v_ref[t, ig * SUBL:(ig + 1) * SUBL, :]
            yt = jnp.zeros((SUBL, LANES), F32)
            for si in range(SUBL):
                i = ig * SUBL + si
                s = [s_scr[i, jb * SUBL:(jb + 1) * SUBL, :] for jb in range(njb)]
                sa = _sublane_allsum(_tree_sum([s[jb] * nkk[jb] for jb in range(njb)]))
                vb = jnp.broadcast_to(vt[si:si + 1, :], (SUBL, LANES))
                s = [s[jb] * w[jb] + sa * kka[jb] + vb * km[jb] for jb in range(njb)]
                for jb in range(njb):
                    s_scr[i, jb * SUBL:(jb + 1) * SUBL, :] = s[jb]
                yb = _sublane_allsum(_tree_sum([s[jb] * r[jb] for jb in range(njb)]))
                yt = jnp.where(sub == si, yb, yt)
            y_ref[t, ig * SUBL:(ig + 1) * SUBL, :] = yt
        return c
    if n_t == 1:
        step(0, 0)
    else:
        lax.fori_loop(0, n_t, step, 0)

    @pl.when(tb == pl.num_programs(1) - 1)
    def _():
        sfin_ref[...] = s_scr[...]


def _lanes_from_keyvec(x, batch, seq):
    g = batch // SCAN_G
    x = x.reshape(g, SCAN_G, seq, RW_H, RW_N).transpose(0, 2, 4, 1, 3).reshape(g, seq, RW_N, SCAN_G * RW_H)
    return jnp.concatenate([x, x], axis=-1)


def _lanes_from_valvec(x, batch, seq):
    g = batch // SCAN_G
    x = x.reshape(g, SCAN_G, seq, RW_H, 2, SCAN_IH).transpose(0, 2, 5, 4, 1, 3)
    return x.reshape(g, seq, SCAN_IH, LANES)


def _valvec_from_lanes(y, batch, seq):
    g = batch // SCAN_G
    y = y.reshape(g, seq, SCAN_IH, 2, SCAN_G, RW_H).transpose(0, 4, 1, 5, 3, 2)
    return y.reshape(batch * seq, RW_W)


def _lanes_from_state(s):
    g = s.shape[0] // SCAN_G
    s = s.reshape(g, SCAN_G, RW_H, 2, SCAN_IH, RW_N).transpose(0, 4, 5, 3, 1, 2)
    return s.reshape(g, SCAN_IH, RW_N, LANES)


def _state_from_lanes(s):
    g = s.shape[0]
    s = s.reshape(g, SCAN_IH, RW_N, 2, SCAN_G, RW_H).transpose(0, 4, 5, 3, 1, 2)
    return s.reshape(g * SCAN_G, RW_H, RW_N, RW_N)


def _wkv_scan(w, nkk, kka, km, r, v, s0_lanes, batch, seq, tb):
    g = batch // SCAN_G
    keyvecs = [_lanes_from_keyvec(a, batch, seq) for a in (w, nkk, kka, km, r)]
    vl = _lanes_from_valvec(v, batch, seq)
    kblk = pl.BlockSpec((None, tb, RW_N, LANES), lambda gi, t: (gi, t, 0, 0))
    vblk = pl.BlockSpec((None, tb, SCAN_IH, LANES), lambda gi, t: (gi, t, 0, 0))
    st = pl.BlockSpec((None, SCAN_IH, RW_N, LANES), lambda gi, t: (gi, 0, 0, 0))
    y, sfin = pl.pallas_call(
        _wkv_scan_kernel,
        grid=(g, seq // tb),
        in_specs=[kblk] * 5 + [vblk, st],
        out_specs=[vblk, st],
        out_shape=[jax.ShapeDtypeStruct((g, seq, SCAN_IH, LANES), F32),
                   jax.ShapeDtypeStruct((g, SCAN_IH, RW_N, LANES), F32)],
        scratch_shapes=[pltpu.VMEM((SCAN_IH, RW_N, LANES), F32)],
        compiler_params=_params(("arbitrary", "arbitrary"), 48),
        name="wkv_scan",
    )(*keyvecs, vl, s0_lanes)
    return _valvec_from_lanes(y, batch, seq), _state_from_lanes(sfin)


def _out_ffn_kernel(x_ref, att_ref, y_ref, bonus_ref, g_ref, gate1_ref, sh2_ref, sc2_ref, gate2_ref,
                    lnw_ref, lnb_ref, n2g_ref, bd_ref, wo_ref, wg_ref, wu_ref, wd_ref, o_ref):
    bd = bd_ref[...]
    y = y_ref[...]
    mu = _split_dot(y, bd) * (1.0 / RW_N)
    d = y - mu
    var = _split_dot(d * d, bd) * (1.0 / RW_N)
    yn = d * lax.rsqrt(var + GN_EPS) * lnw_ref[...] + lnb_ref[...]
    rw = (yn + bonus_ref[...]) * g_ref[...]
    mix = (jnp.dot(att_ref[...].astype(BF), wo_ref[0:ATT_W, :], preferred_element_type=F32)
           + jnp.dot(rw.astype(BF), wo_ref[ATT_W:, :], preferred_element_type=F32))
    x1 = x_ref[...] + gate1_ref[...] * mix
    ms = jnp.mean(x1 * x1, axis=-1, keepdims=True)
    h2 = x1 * lax.rsqrt(ms + RMS_EPS) * n2g_ref[...]
    hb = (h2 * (1.0 + sc2_ref[...]) + sh2_ref[...]).astype(BF)
    a = jnp.dot(hb, wg_ref[...], preferred_element_type=F32)
    u = jnp.dot(hb, wu_ref[...], preferred_element_type=F32)
    f = (a * _sigmoid(a)) * u
    ffn = jnp.dot(f.astype(BF), wd_ref[...], preferred_element_type=F32)
    o_ref[...] = x1 + gate2_ref[...] * ffn


def _out_ffn(x2d, att, y, bonus, g, mod, per_batch, tiles_per_batch, tm,
             ln_w, ln_b, norm2_g, bd512, wo, wg, wu, wd):
    n, d = x2d.shape
    row = lambda a: pl.BlockSpec((tm, a.shape[1]), lambda i: (i, 0))
    const = lambda a: pl.BlockSpec(a.shape, lambda i: (0,) * a.ndim, pipeline_mode=pl.Buffered(1))
    ms = lambda c: _mod_spec(per_batch, c, tiles_per_batch, tm, d)
    return pl.pallas_call(
        _out_ffn_kernel,
        grid=(n // tm,),
        in_specs=[row(x2d), row(att), row(y), row(bonus), row(g), ms(2), ms(3), ms(4), ms(5)]
                 + [const(a) for a in (ln_w, ln_b, norm2_g, bd512, wo, wg, wu, wd)],
        out_specs=pl.BlockSpec((tm, d), lambda i: (i, 0)),
        out_shape=jax.ShapeDtypeStruct((n, d), F32),
        compiler_params=_params(("arbitrary",), 56),
        name="out_ffn",
    )(x2d, att, y, bonus, g, mod, mod, mod, mod, ln_w, ln_b, norm2_g, bd512, wo, wg, wu, wd)


def _block_diag_ones(width, seg):
    i = jnp.arange(width)
    return (i[:, None] // seg == i[None, :] // seg).astype(BF)


def kernel(x_prompt, x_sample, cache_k, cache_v, cache_idx_k, state_wkv, state_shift, page_table,
           c_prompt, c_sample, norm1_g, norm2_g, w_ada, b_ada, w_in, q_norm_g, k_norm_g,
           rw_mu, rw_w0, rw_w2, rw_a0, rw_a2, rw_g2, rw_k_k, rw_k_a, rw_r_k, rw_ln_w, rw_ln_b,
           w_out, w_ffn_gate, w_ffn_up, w_ffn_down):
    bp, seq, d = x_prompt.shape
    bs, dec_seq, _ = x_sample.shape
    assert dec_seq == 1 and w_ada.shape[0] == 1
    n_pages = page_table.shape[1]
    past = n_pages * PAGE_SIZE
    n_pool = cache_k.shape[1]
    np_tok = bp * seq

    wi = w_in[0]
    w_in_p = jnp.concatenate([wi[:, :ATT_COLS], jnp.zeros((d, C_PR - ATT_COLS), F32), wi[:, ATT_COLS:]],
                             axis=1).astype(BF)
    tile_h = lambda g: jnp.tile(g.reshape(1, HEAD_DIM), (1, ATT_H))
    qg, kg = tile_h(q_norm_g[0]), tile_h(k_norm_g[0])
    bd512 = _block_diag_ones(ATT_W, HEAD_DIM)
    row = lambda a: a.reshape(1, -1)
    w2p = jnp.concatenate([rw_w2[0], jnp.zeros((A_LORA, RW_W), F32)], axis=0).astype(BF)
    a2p = jnp.concatenate([jnp.zeros((W_LORA, RW_W), F32), rw_a2[0]], axis=0).astype(BF)
    g2b = rw_g2[0].astype(BF)
    wo, wg, wu, wd = (w_out[0].astype(BF), w_ffn_gate[0].astype(BF), w_ffn_up[0].astype(BF),
                      w_ffn_down[0].astype(BF))

    mod = _adaln(jnp.concatenate([c_sample, c_prompt], axis=0), w_ada[0], b_ada[0])
    mod_s = mod[:bs]
    mod_p = mod[bs:].reshape(bp, 1, 6 * d)

    def rwkv_and_out(x2d, att, pr, prev, shift_in_kernel, s0_packed, batch, t_len, tb, mod_x, per_batch,
                     tiles_per_batch, tm):
        w, nkk, kka, km, r, v, g, bonus = _rw_prep(
            pr, prev, shift_in_kernel, tiles_per_batch, tm, row(rw_mu[0]), row(rw_w0[0]), w2p, row(rw_a0[0]),
            a2p, g2b, row(rw_k_k[0]), row(rw_k_a[0]), row(rw_r_k[0]), bd512)
        y, sfin = _wkv_scan(w, nkk, kka, km, r, v, s0_packed, batch, t_len, tb)
        out = _out_ffn(x2d, att, y, bonus, g, mod_x, per_batch, tiles_per_batch, tm,
                       row(rw_ln_w[0]), row(rw_ln_b[0]), row(norm2_g[0]), bd512, wo, wg, wu, wd)
        return out, sfin

    tm = 256
    tpb = seq // tm
    xp = x_prompt.reshape(np_tok, d)
    (q_bf, k_p, v_p, ke, ko, ve, vo, qi_bf, kiw_p, kie, kio, pr_p) = _inproj(
        xp, mod_p, True, tpb, tm, row(norm1_g[0]), w_in_p, qg, kg, bd512)
    att_p = _attn_prompt(q_bf, qi_bf, kiw_p, ke, ko, ve, vo, kie, kio, bp, seq)
    y_p, wkv_p = rwkv_and_out(xp, att_p, pr_p, jnp.zeros((bp, 1, RW_COLS), F32), True,
                              jnp.zeros((bp // SCAN_G, SCAN_IH, RW_N, LANES), F32), bp, seq, 64, mod_p, True, tpb, tm)

    xs = x_sample.reshape(bs, d)
    (q_s, k_s, v_s, _, _, _, _, qi_s, kiw_s, _, _, pr_s) = _inproj(
        xs, mod_s, False, 1, bs, row(norm1_g[0]), w_in_p, qg, kg, bd512)
    page_flat = page_table.reshape(-1).astype(I32)
    scores = _samp_scores(page_flat, qi_s.reshape(bs, IDX_H, IDX_D),
                          kiw_s[:, IDX_D:IDX_D + IDX_H].reshape(bs, IDX_H, 1), cache_idx_k[0], n_pages)
    ii = jnp.arange(LANES)[:, None]
    jj = jnp.arange(IDX_H * IDX_D)[None, :]
    rep = ((ii < IDX_D) & (jj % IDX_D == ii)).astype(BF)
    hsum = (jnp.arange(IDX_H * IDX_D)[:, None] // IDX_D == jnp.arange(LANES)[None, :]).astype(BF)
    bias = _samp_select(scores.reshape(bs, past), qi_s, kiw_s, rep, hsum, past)
    expand = (jnp.arange(LANES)[:, None] == jnp.arange(ATT_W)[None, :] // HEAD_DIM).astype(BF)
    att_s = _samp_attn(page_flat, q_s.reshape(bs, 1, ATT_W), bias.reshape(bs, n_pages + 1, LANES),
                       k_s.reshape(bs, 1, ATT_W), v_s.reshape(bs, 1, ATT_W), expand,
                       cache_k[0].reshape(n_pool, PAGE_SIZE, ATT_W), cache_v[0].reshape(n_pool, PAGE_SIZE, ATT_W),
                       n_pages).reshape(bs, ATT_W)
    y_s, wkv_s = rwkv_and_out(xs, att_s, pr_s, state_shift[0], False, _lanes_from_state(state_wkv[0]),
                              bs, 1, 1, mod_s, False, 1, bs)

    hd = (ATT_H, HEAD_DIM)
    return (y_p.reshape(bp, seq, d), y_s.reshape(bs, 1, d),
            k_p.reshape((1, bp, seq) + hd), v_p.reshape((1, bp, seq) + hd),
            kiw_p[:, :IDX_D].reshape(1, bp, seq, IDX_D),
            wkv_p[None], pr_p.reshape(bp, seq, RW_COLS)[:, -1][None],
            k_s.reshape((1, bs, 1) + hd), v_s.reshape((1, bs, 1) + hd),
            kiw_s[:, :IDX_D].reshape(1, bs, 1, IDX_D),
            wkv_s[None], pr_s[None])
```
